```python
import math
import jax
import jax.numpy as jnp
from jax import lax
import numpy as np

D_MODEL = 1024
BATCH = 2
SEQ = 8192
DEPTH = 2
DEC_BATCH = 32
DEC_SEQ = 1
PAST_LEN = 16384
PAGE_SIZE = 128

N_EVEN = (DEPTH + 1) // 2
N_ODD = DEPTH // 2
D_CONV = D_MODEL // 2
D_POOL = D_MODEL - D_CONV
CONV_WIDTH = 3
POOL_WINDOWS = (2, 4, 8, 16)
N_POOL_GROUPS = len(POOL_WINDOWS)
POOL_GROUP = D_POOL // N_POOL_GROUPS
POOL_STATE = max(POOL_WINDOWS) - 1
D_IN_SP = 3 * D_CONV + D_POOL
N_HEADS = 16
HEAD_DIM = D_MODEL // N_HEADS
MOBA_BLOCK = 256
MOBA_TOP = 3
Q_BLOCK = 64
ROPE_THETA = 10000.0
D_FF = 2816
N_EXPERTS = 8
TOP_K = 2
D_EXPERT = 3584
D_PLE = 256
RMS_EPS = 1e-6

kernel_name = 'hybrid_conv_pool_moba_decoder'


def rms_norm(x, g):
    xf = x.astype(jnp.float32)
    y = xf * lax.rsqrt(jnp.mean(xf * xf, axis=-1, keepdims=True) + RMS_EPS)
    return (y * g.astype(jnp.float32)).astype(x.dtype)


def rope(x, pos):
    half = HEAD_DIM // 2
    inv = ROPE_THETA ** (-jnp.arange(half, dtype=jnp.float32) / half)
    ang = pos.astype(jnp.float32)[:, None] * inv[None, :]
    cos = jnp.cos(ang)[None, :, None, :]
    sin = jnp.sin(ang)[None, :, None, :]
    xf = x.astype(jnp.float32)
    x1, x2 = xf[..., :half], xf[..., half:]
    return jnp.concatenate([x1 * cos - x2 * sin, x2 * cos + x1 * sin], axis=-1).astype(x.dtype)


def swiglu(x, w_gate, w_up, w_down):
    return (jax.nn.silu(x @ w_gate) * (x @ w_up)) @ w_down


def causal_multi_pool(v, n_out):
    B, T, _ = v.shape
    vg = v.astype(jnp.float32).reshape(B, T, N_POOL_GROUPS, POOL_GROUP)
    cs = jnp.concatenate([jnp.zeros((B, 1, N_POOL_GROUPS, POOL_GROUP), jnp.float32),
                          jnp.cumsum(vg, axis=1)], axis=1)
    t = jnp.arange(T - n_out, T)
    means = []
    for g, w in enumerate(POOL_WINDOWS):
        lo = jnp.maximum(t + 1 - w, 0)
        cnt = (t + 1 - lo).astype(jnp.float32)
        means.append((cs[:, t + 1, g] - cs[:, lo, g]) / cnt[None, :, None])
    mean = jnp.stack(means, axis=2)
    return (mean - vg[:, T - n_out:]).astype(v.dtype)


def conv_pool_mixer(h, conv_prev, pool_prev, w_in, conv_w, w_pool, pool_scale, w_out):
    B, L, _ = h.shape
    z = h @ w_in
    a_h, a_b, a_c, b_in = jnp.split(z, [D_CONV, 2 * D_CONV, 3 * D_CONV], axis=-1)
    u = a_c * a_h
    up = jnp.concatenate([conv_prev.astype(u.dtype), u], axis=1)
    conv = sum(conv_w[j] * up[:, j:j + L] for j in range(CONV_WIDTH))
    y_a = a_b * conv
    vp = jnp.concatenate([pool_prev.astype(b_in.dtype), b_in], axis=1)
    d = causal_multi_pool(vp, L)
    y_b = jnp.einsum('blgc,gcd->blgd', d, w_pool).reshape(B, L, D_POOL) * pool_scale
    out = jnp.concatenate([y_a, y_b], axis=-1) @ w_out
    return out, up[:, -(CONV_WIDTH - 1):], vp[:, -POOL_STATE:]


def moba_chunk(q, pos, kb, vb, kmean, n_sel):
    B, NB = kb.shape[0], kb.shape[1]
    Qc = q.shape[1]
    own = pos // MOBA_BLOCK
    gate = jnp.einsum('bqhd,bnhd->bqhn', q.astype(jnp.float32), kmean)
    past = jnp.arange(NB)[None, :] < own[:, None]
    gate = jnp.where(past[None, :, None, :], gate, -jnp.inf)
    _, sel = lax.top_k(gate, n_sel)
    own_b = jnp.broadcast_to(own[None, :, None, None], (B, Qc, N_HEADS, 1)).astype(sel.dtype)
    blocks = jnp.concatenate([sel, own_b], axis=-1)
    bi = jnp.arange(B)[:, None, None, None]
    hi = jnp.arange(N_HEADS)[None, None, :, None]
    kg = kb[bi, blocks, :, hi]
    vg = vb[bi, blocks, :, hi]
    s = jnp.einsum('bqhd,bqhnkd->bqhnk', q, kg).astype(jnp.float32) * (HEAD_DIM ** -0.5)
    key_pos = blocks[..., None] * MOBA_BLOCK + jnp.arange(MOBA_BLOCK)
    causal = key_pos <= pos[None, :, None, None, None]
    sel_ok = blocks[..., None] < own[None, :, None, None, None]
    is_sel = (jnp.arange(n_sel + 1) < n_sel)[None, None, None, :, None]
    s = jnp.where(jnp.where(is_sel, sel_ok, causal), s, -jnp.inf)
    shp = s.shape
    p = jax.nn.softmax(s.reshape(shp[0], shp[1], shp[2], -1), axis=-1).reshape(shp)
    return jnp.einsum('bqhnk,bqhnkd->bqhd', p.astype(vg.dtype), vg)


def moba_attention(h, pos, k_prev, v_prev, w_qkv, w_o):
    B, L, _ = h.shape
    qkv = (h @ w_qkv).reshape(B, L, 3, N_HEADS, HEAD_DIM)
    q = rope(qkv[:, :, 0], pos)
    k = rope(qkv[:, :, 1], pos)
    v = qkv[:, :, 2]
    n_prev = 0 if k_prev is None else k_prev.shape[1]
    T = n_prev + L
    n_blk = -(-T // MOBA_BLOCK)
    pad = jnp.zeros((B, n_blk * MOBA_BLOCK - T, N_HEADS, HEAD_DIM), k.dtype)
    if k_prev is None:
        k_all = jnp.concatenate([k, pad], axis=1)
        v_all = jnp.concatenate([v, pad], axis=1)
    else:
        k_all = jnp.concatenate([k_prev.astype(k.dtype), k, pad], axis=1)
        v_all = jnp.concatenate([v_prev.astype(v.dtype), v, pad], axis=1)
    kb = k_all.reshape(B, n_blk, MOBA_BLOCK, N_HEADS, HEAD_DIM)
    vb = v_all.reshape(B, n_blk, MOBA_BLOCK, N_HEADS, HEAD_DIM)
    kmean = jnp.mean(kb.astype(jnp.float32), axis=2)
    n_sel = min(MOBA_TOP, n_blk)
    qb = Q_BLOCK if L % Q_BLOCK == 0 else L
    nq = L // qb
    qc = q.reshape(B, nq, qb, N_HEADS, HEAD_DIM).transpose(1, 0, 2, 3, 4)
    pc = pos.reshape(nq, qb)
    o = lax.map(lambda a: moba_chunk(a[0], a[1], kb, vb, kmean, n_sel), (qc, pc))
    o = o.transpose(1, 0, 2, 3, 4).reshape(B, L, D_MODEL)
    return o @ w_o, k, v


def moe_swiglu(x, w_router, b_router, w_gate, w_up, w_down):
    shp = x.shape
    xf = x.reshape(-1, shp[-1])
    logits = (xf @ w_router).astype(jnp.float32) + b_router.astype(jnp.float32)
    probs = jax.nn.softmax(logits, axis=-1)
    top_p, top_i = lax.top_k(probs, TOP_K)
    top_p = top_p / jnp.sum(top_p, axis=-1, keepdims=True)
    comb = jnp.sum(jax.nn.one_hot(top_i, N_EXPERTS, dtype=jnp.float32) * top_p[..., None], axis=1)
    out = jnp.zeros(xf.shape, jnp.float32)
    for e in range(N_EXPERTS):
        out = out + comb[:, e:e + 1] * swiglu(xf, w_gate[e], w_up[e], w_down[e]).astype(jnp.float32)
    return out.astype(x.dtype).reshape(shp)


def per_layer_embed(x, p, g, w_proj, w_gate):
    gate = jax.nn.sigmoid((rms_norm(x, g) @ w_gate).astype(jnp.float32))
    return ((p @ w_proj).astype(jnp.float32) * gate).astype(x.dtype)


def gather_pages(cache, page_table):
    pages = cache[page_table]
    return pages.reshape(page_table.shape[0], -1, N_HEADS, HEAD_DIM)


def trunk(x, p, pos, conv_prev, pool_prev, kv_past, params):
    (g_mix, g_ffn, g_ple, g_final, w_in_sp, conv_w, w_pool_grp, pool_scale, w_out_sp,
     w_ffn_gate, w_ffn_up, w_ffn_down, w_qkv, w_o, w_router, b_router,
     w_exp_gate, w_exp_up, w_exp_down, w_ple_proj, w_ple_gate) = params
    convs, pools, ks, vs = [], [], [], []
    for i in range(DEPTH):
        j = i // 2
        h = rms_norm(x, g_mix[i])
        if i % 2 == 0:
            mix, c_new, p_new = conv_pool_mixer(h, conv_prev[j], pool_prev[j], w_in_sp[j], conv_w[j],
                                                w_pool_grp[j], pool_scale[j], w_out_sp[j])
            convs.append(c_new)
            pools.append(p_new)
            x = x + mix
            x = x + swiglu(rms_norm(x, g_ffn[i]), w_ffn_gate[j], w_ffn_up[j], w_ffn_down[j])
        else:
            if kv_past is None:
                k_prev, v_prev = None, None
            else:
                cache_k, cache_v, page_table = kv_past
                k_prev = gather_pages(cache_k[j], page_table)
                v_prev = gather_pages(cache_v[j], page_table)
            mix, k_new, v_new = moba_attention(h, pos, k_prev, v_prev, w_qkv[j], w_o[j])
            ks.append(k_new)
            vs.append(v_new)
            x = x + mix
            x = x + moe_swiglu(rms_norm(x, g_ffn[i]), w_router[j], b_router[j],
                               w_exp_gate[j], w_exp_up[j], w_exp_down[j])
        x = x + per_layer_embed(x, p[i], g_ple[i], w_ple_proj[i], w_ple_gate[i])
    return rms_norm(x, g_final), jnp.stack(convs), jnp.stack(pools), jnp.stack(ks), jnp.stack(vs)


def setup_inputs(seed: int = 0) -> dict:
    key = jax.random.key(seed)
    ks = jax.random.split(key, 32)
    f32 = jnp.float32

    def w(k, shape, fan_in):
        return jax.random.normal(k, shape, f32) * (fan_in ** -0.5)

    def gain(k, shape):
        return 1.0 + 0.1 * jax.random.normal(k, shape, f32)

    n_pages = PAST_LEN // PAGE_SIZE
    n_used = DEC_BATCH * n_pages
    n_phys = (5 * n_used + 3) // 4
    page_table = jax.random.permutation(ks[0], n_phys)[:n_used].reshape(DEC_BATCH, n_pages).astype(jnp.int32)
    return {
        'x_prompt': jax.random.normal(ks[1], (BATCH, SEQ, D_MODEL), f32),
        'x_sample': jax.random.normal(ks[2], (DEC_BATCH, DEC_SEQ, D_MODEL), f32),
        'state_conv': jax.random.normal(ks[3], (N_EVEN, DEC_BATCH, CONV_WIDTH - 1, D_CONV), f32),
        'state_pool': jax.random.normal(ks[4], (N_EVEN, DEC_BATCH, POOL_STATE, D_POOL), f32),
        'cache_k': jax.random.normal(ks[5], (N_ODD, n_phys, PAGE_SIZE, N_HEADS, HEAD_DIM), f32),
        'cache_v': jax.random.normal(ks[6], (N_ODD, n_phys, PAGE_SIZE, N_HEADS, HEAD_DIM), f32),
        'page_table': page_table,
        'p_prompt': jax.random.normal(ks[7], (DEPTH, BATCH, SEQ, D_PLE), f32),
        'p_sample': jax.random.normal(ks[8], (DEPTH, DEC_BATCH, DEC_SEQ, D_PLE), f32),
        'g_mix': gain(ks[9], (DEPTH, D_MODEL)),
        'g_ffn': gain(ks[10], (DEPTH, D_MODEL)),
        'g_ple': gain(ks[11], (DEPTH, D_MODEL)),
        'g_final': gain(ks[12], (D_MODEL,)),
        'w_in_sp': w(ks[13], (N_EVEN, D_MODEL, D_IN_SP), D_MODEL),
        'conv_w': w(ks[14], (N_EVEN, CONV_WIDTH, D_CONV), CONV_WIDTH),
        'w_pool_grp': w(ks[15], (N_EVEN, N_POOL_GROUPS, POOL_GROUP, POOL_GROUP), POOL_GROUP),
        'pool_scale': gain(ks[16], (N_EVEN, D_POOL)),
        'w_out_sp': w(ks[17], (N_EVEN, D_CONV + D_POOL, D_MODEL), D_CONV + D_POOL),
        'w_ffn_gate': w(ks[18], (N_EVEN, D_MODEL, D_FF), D_MODEL),
        'w_ffn_up': w(ks[19], (N_EVEN, D_MODEL, D_FF), D_MODEL),
        'w_ffn_down': w(ks[20], (N_EVEN, D_FF, D_MODEL), D_FF),
        'w_qkv': w(ks[21], (N_ODD, D_MODEL, 3 * D_MODEL), D_MODEL),
        'w_o': w(ks[22], (N_ODD, D_MODEL, D_MODEL), D_MODEL),
        'w_router': w(ks[23], (N_ODD, D_MODEL, N_EXPERTS), D_MODEL),
        'b_router': 0.01 * jax.random.normal(ks[24], (N_ODD, N_EXPERTS), f32),
        'w_exp_gate': w(ks[25], (N_ODD, N_EXPERTS, D_MODEL, D_EXPERT), D_MODEL),
        'w_exp_up': w(ks[26], (N_ODD, N_EXPERTS, D_MODEL, D_EXPERT), D_MODEL),
        'w_exp_down': w(ks[27], (N_ODD, N_EXPERTS, D_EXPERT, D_MODEL), D_EXPERT),
        'w_ple_proj': w(ks[28], (DEPTH, D_PLE, D_MODEL), D_PLE),
        'w_ple_gate': w(ks[29], (DEPTH, D_MODEL, D_MODEL), D_MODEL),
    }


def reference(x_prompt, x_sample, state_conv, state_pool, cache_k, cache_v, page_table,
              p_prompt, p_sample, g_mix, g_ffn, g_ple, g_final, w_in_sp, conv_w, w_pool_grp,
              pool_scale, w_out_sp, w_ffn_gate, w_ffn_up, w_ffn_down, w_qkv, w_o, w_router,
              b_router, w_exp_gate, w_exp_up, w_exp_down, w_ple_proj, w_ple_gate):
    params = (g_mix, g_ffn, g_ple, g_final, w_in_sp, conv_w, w_pool_grp, pool_scale, w_out_sp,
              w_ffn_gate, w_ffn_up, w_ffn_down, w_qkv, w_o, w_router, b_router,
              w_exp_gate, w_exp_up, w_exp_down, w_ple_proj, w_ple_gate)
    b_p, l_p = x_prompt.shape[0], x_prompt.shape[1]
    past = page_table.shape[1] * PAGE_SIZE
    pos_p = jnp.arange(l_p, dtype=jnp.int32)
    pos_s = past + jnp.arange(x_sample.shape[1], dtype=jnp.int32)
    conv0 = jnp.zeros((N_EVEN, b_p, CONV_WIDTH - 1, D_CONV), x_prompt.dtype)
    pool0 = jnp.zeros((N_EVEN, b_p, 0, D_POOL), x_prompt.dtype)
    y_p, conv_p, pool_p, k_p, v_p = trunk(x_prompt, p_prompt, pos_p, conv0, pool0, None, params)
    y_s, conv_s, pool_s, k_s, v_s = trunk(x_sample, p_sample, pos_s, state_conv, state_pool,
                                          (cache_k, cache_v, page_table), params)
    return (y_p, y_s, conv_p, conv_s, pool_p, pool_s, k_p, v_p, k_s, v_s)
```

```python
import functools

import numpy as np
import jax
import jax.numpy as jnp
from jax import lax
from jax.experimental import pallas as pl
from jax.experimental.pallas import tpu as pltpu

F32 = jnp.float32
BF16 = jnp.bfloat16

D_MODEL = 1024
D_CONV = 512
D_POOL = 512
CONV_WIDTH = 3
POOL_WINDOWS = (2, 4, 8, 16)
POOL_GROUP = 128
POOL_STATE = 15
N_HEADS = 16
HEAD_DIM = 64
MOBA_BLOCK = 256
MOBA_TOP = 3
ROPE_THETA = 10000.0
D_FF = 2816
N_EXPERTS = 8
D_EXPERT = 3584
D_PLE = 256
RMS_EPS = 1e-6
PAGE_SIZE = 128

LANES = 128
HALO = 16
HEADS_PER_STRIP = LANES // HEAD_DIM
N_STRIPS = D_MODEL // LANES
PAGES_PER_BLOCK = MOBA_BLOCK // PAGE_SIZE
AUG = 2 * LANES
NEG_BIG = -2.0 ** 100
KEY_GROUP = 4
FF_CHUNKS = ((0, 768), (768, 1536), (1536, 2304), (2304, 2816))
EXPERT_TILE = 512
VMEM_LIMIT = 56 * 1024 * 1024


def _params(*sem):
    return pltpu.CompilerParams(dimension_semantics=sem, vmem_limit_bytes=VMEM_LIMIT)


def _resident(shape):
    nd = len(shape)
    return pl.BlockSpec(shape, lambda *_: (0,) * nd, pipeline_mode=pl.Buffered(1))


def _rows(tm, width):
    return pl.BlockSpec((tm, width), lambda t: (t, 0))


def _dot(a, b):
    return jnp.dot(a, b, preferred_element_type=F32)


def _mm(a, w):
    if w.dtype == BF16:
        return jnp.dot(a.astype(BF16), w, preferred_element_type=F32)
    return jnp.dot(a.astype(F32), w, precision=lax.Precision.HIGHEST, preferred_element_type=F32)


def _dot_nt(a, b, precision=None):
    return lax.dot_general(a, b, (((1,), (1,)), ((), ())), precision=precision,
                           preferred_element_type=F32)


def _rms(x, g):
    return x * lax.rsqrt(jnp.mean(x * x, axis=-1, keepdims=True) + RMS_EPS) * g


def _silu(x):
    return x * jax.nn.sigmoid(x)


def _lane_iota(shape):
    return lax.broadcasted_iota(jnp.int32, shape, len(shape) - 1)


def _mixer_tail(x, a_b, b_in, conv, pooled, wpool_ref, ps_ref, wout_ref):
    parts = [a_b * conv]
    for g in range(len(POOL_WINDOWS)):
        cols = slice(g * POOL_GROUP, (g + 1) * POOL_GROUP)
        d = pooled[g] - b_in[:, cols]
        parts.append(_mm(d, wpool_ref[g]) * ps_ref[:, cols])
    return x + _mm(jnp.concatenate(parts, axis=-1), wout_ref[...])


def _mixer_prompt_body(x_ref, g_ref, win_ref, cw_ref, wpool_ref, ps_ref, wout_ref,
                       xo_ref, ut_ref, bt_ref, uext, bext, *, tm, tiles_per_seq):
    tin = pl.program_id(0) % tiles_per_seq
    x = x_ref[...]
    z = _mm(_rms(x, g_ref[...]), win_ref[...])
    a_h = z[:, 0:D_CONV]
    a_b = z[:, D_CONV:2 * D_CONV]
    a_c = z[:, 2 * D_CONV:3 * D_CONV]
    b_in = z[:, 3 * D_CONV:]
    u = a_c * a_h

    @pl.when(tin == 0)
    def _():
        uext[0:HALO, :] = jnp.zeros((HALO, D_CONV), F32)
        bext[0:HALO, :] = jnp.zeros((HALO, D_POOL), F32)

    @pl.when(tin != 0)
    def _():
        uext[0:HALO, :] = uext[tm:tm + HALO, :]
        bext[0:HALO, :] = bext[tm:tm + HALO, :]

    uext[HALO:HALO + tm, :] = u
    bext[HALO:HALO + tm, :] = b_in

    cw = cw_ref[...]
    conv = (cw[0:1, :] * uext[HALO - 2:HALO - 2 + tm, :]
            + cw[1:2, :] * uext[HALO - 1:HALO - 1 + tm, :]
            + cw[2:3, :] * u)

    pos = tin * tm + lax.broadcasted_iota(jnp.int32, (tm, 1), 0)
    pooled = []
    for g, w in enumerate(POOL_WINDOWS):
        cols = slice(g * POOL_GROUP, (g + 1) * POOL_GROUP)
        s = b_in[:, cols]
        for r in range(1, w):
            s = s + bext[HALO - r:HALO - r + tm, cols]
        cnt = jnp.minimum(pos + 1, w).astype(F32)
        pooled.append(s / cnt)

    xo_ref[...] = _mixer_tail(x, a_b, b_in, conv, pooled, wpool_ref, ps_ref, wout_ref)
    ut_ref[0] = uext[tm:tm + HALO, :]
    bt_ref[0] = bext[tm:tm + HALO, :]


def _mixer_prompt(x, g, win, cw, wpool, ps, wout, *, tm, seq):
    n = x.shape[0]
    nt = n // tm
    body = functools.partial(_mixer_prompt_body, tm=tm, tiles_per_seq=seq // tm)
    return pl.pallas_call(
        body,
        grid=(nt,),
        in_specs=[_rows(tm, D_MODEL), _resident(g.shape), _resident(win.shape), _resident(cw.shape),
                  _resident(wpool.shape), _resident(ps.shape), _resident(wout.shape)],
        out_specs=[_rows(tm, D_MODEL),
                   pl.BlockSpec((1, HALO, D_CONV), lambda t: (t, 0, 0)),
                   pl.BlockSpec((1, HALO, D_POOL), lambda t: (t, 0, 0))],
        out_shape=[jax.ShapeDtypeStruct((n, D_MODEL), F32),
                   jax.ShapeDtypeStruct((nt, HALO, D_CONV), F32),
                   jax.ShapeDtypeStruct((nt, HALO, D_POOL), F32)],
        scratch_shapes=[pltpu.VMEM((HALO + tm, D_CONV), F32), pltpu.VMEM((HALO + tm, D_POOL), F32)],
        compiler_params=_params("arbitrary"),
        name="mixer_prompt",
    )(x, g, win, cw, wpool, ps, wout)


def _mixer_sample_body(x_ref, sc_ref, sp_ref, g_ref, win_ref, cw_ref, wpool_ref, ps_ref, wout_ref,
                       xo_ref, co_ref, po_ref):
    x = x_ref[...]
    z = _mm(_rms(x, g_ref[...]), win_ref[...])
    a_h = z[:, 0:D_CONV]
    a_b = z[:, D_CONV:2 * D_CONV]
    a_c = z[:, 2 * D_CONV:3 * D_CONV]
    b_in = z[:, 3 * D_CONV:]
    u = a_c * a_h
    cw = cw_ref[...]
    conv = cw[0:1, :] * sc_ref[:, 0:D_CONV] + cw[1:2, :] * sc_ref[:, D_CONV:] + cw[2:3, :] * u
    pooled = []
    for g, w in enumerate(POOL_WINDOWS):
        s = b_in[:, g * POOL_GROUP:(g + 1) * POOL_GROUP]
        for r in range(1, w):
            c0 = (POOL_STATE - r) * D_POOL + g * POOL_GROUP
            s = s + sp_ref[:, c0:c0 + POOL_GROUP]
        pooled.append(s / float(w))
    xo_ref[...] = _mixer_tail(x, a_b, b_in, conv, pooled, wpool_ref, ps_ref, wout_ref)
    co_ref[:, 0:D_CONV] = sc_ref[:, D_CONV:]
    co_ref[:, D_CONV:] = u
    po_ref[:, 0:(POOL_STATE - 1) * D_POOL] = sp_ref[:, D_POOL:]
    po_ref[:, (POOL_STATE - 1) * D_POOL:] = b_in


def _mixer_sample(x, sc, sp, g, win, cw, wpool, ps, wout):
    n = x.shape[0]
    return pl.pallas_call(
        _mixer_sample_body,
        out_shape=[jax.ShapeDtypeStruct((n, D_MODEL), F32),
                   jax.ShapeDtypeStruct(sc.shape, F32),
                   jax.ShapeDtypeStruct(sp.shape, F32)],
        compiler_params=pltpu.CompilerParams(vmem_limit_bytes=VMEM_LIMIT),
        name="mixer_sample",
    )(x, sc, sp, g, win, cw, wpool, ps, wout)


def _ple(x, p_ref, g_ref, wpp_ref, wpg_ref):
    gate = jax.nn.sigmoid(_mm(_rms(x, g_ref[...]), wpg_ref[...]))
    return x + _mm(p_ref[...], wpp_ref[...]) * gate


def _ffn_ple_body(x_ref, p_ref, gf_ref, wg_ref, wu_ref, wd_ref, gp_ref, wpp_ref, wpg_ref, o_ref):
    x = x_ref[...]
    h = _rms(x, gf_ref[...]).astype(wg_ref.dtype)
    ff = None
    for c0, c1 in FF_CHUNKS:
        a = _silu(_mm(h, wg_ref[:, c0:c1])) * _mm(h, wu_ref[:, c0:c1])
        y = _mm(a, wd_ref[c0:c1, :])
        ff = y if ff is None else ff + y
    o_ref[...] = _ple(x + ff, p_ref, gp_ref, wpp_ref, wpg_ref)


def _ffn_ple(x, p, gf, wg, wu, wd, gp, wpp, wpg, *, tm):
    n = x.shape[0]
    return pl.pallas_call(
        _ffn_ple_body,
        grid=(n // tm,),
        in_specs=[_rows(tm, D_MODEL), _rows(tm, D_PLE), _resident(gf.shape), _resident(wg.shape),
                  _resident(wu.shape), _resident(wd.shape), _resident(gp.shape), _resident(wpp.shape),
                  _resident(wpg.shape)],
        out_specs=_rows(tm, D_MODEL),
        out_shape=jax.ShapeDtypeStruct((n, D_MODEL), F32),
        compiler_params=_params("parallel"),
        name="ffn_ple",
    )(x, p, gf, wg, wu, wd, gp, wpp, wpg)


def _ple_final_body(x_ref, p_ref, gp_ref, wpp_ref, wpg_ref, gfin_ref, o_ref):
    o_ref[...] = _rms(_ple(x_ref[...], p_ref, gp_ref, wpp_ref, wpg_ref), gfin_ref[...])


def _ple_final(x, p, gp, wpp, wpg, gfin, *, tm):
    n = x.shape[0]
    return pl.pallas_call(
        _ple_final_body,
        grid=(n // tm,),
        in_specs=[_rows(tm, D_MODEL), _rows(tm, D_PLE), _resident(gp.shape), _resident(wpp.shape),
                  _resident(wpg.shape), _resident(gfin.shape)],
        out_specs=_rows(tm, D_MODEL),
        out_shape=jax.ShapeDtypeStruct((n, D_MODEL), F32),
        compiler_params=_params("parallel"),
        name="ple_final",
    )(x, p, gp, wpp, wpg, gfin)


def _rope_strip(xs, cos, sin_signed, first_half):
    fwd = pltpu.roll(xs, LANES - HEAD_DIM // 2, axis=1)
    bwd = pltpu.roll(xs, HEAD_DIM // 2, axis=1)
    return xs * cos + jnp.where(first_half, fwd, bwd) * sin_signed


def _qkv_body(x_ref, g_ref, w_ref, inv_ref, q_ref, k_ref, v_ref, *rest, tm, tiles_per_seq, pos0):
    x = x_ref[...]
    qkv = _mm(_rms(x, g_ref[...]), w_ref[...])
    if tiles_per_seq is None:
        pos = jnp.full((tm, 1), pos0, jnp.int32)
    else:
        pos = pos0 + (pl.program_id(0) % tiles_per_seq) * tm + lax.broadcasted_iota(jnp.int32, (tm, 1), 0)
    ang = pos.astype(F32) * inv_ref[...]
    cos = jnp.cos(ang)
    sin = jnp.sin(ang)
    lane = _lane_iota((1, LANES))
    first_half = jnp.bitwise_and(lane, HEAD_DIM - 1) < HEAD_DIM // 2
    sin_signed = jnp.where(first_half, -sin, sin)
    v = qkv[:, 2 * D_MODEL:]
    v_ref[...] = v
    if tiles_per_seq is not None:
        kaug_ref, vaug_ref, km_ref = rest
        head_a = lane < HEAD_DIM
        block_onehot = jnp.where(lane == lax.shift_right_logical(pos, MOBA_BLOCK.bit_length() - 1), 1.0, 0.0)
        block_onehot = block_onehot.astype(BF16)
        unit = jnp.where(lane == HEAD_DIM, 1.0, 0.0)
    for s in range(N_STRIPS):
        cols = slice(s * LANES, (s + 1) * LANES)
        q_ref[:, cols] = _rope_strip(qkv[:, s * LANES:(s + 1) * LANES], cos, sin_signed, first_half)
        ks = _rope_strip(qkv[:, D_MODEL + s * LANES:D_MODEL + (s + 1) * LANES], cos, sin_signed, first_half)
        k_ref[:, cols] = ks
        if tiles_per_seq is not None:
            vs = v[:, s * LANES:(s + 1) * LANES]
            kaug_ref[:, s * AUG:s * AUG + LANES] = ks.astype(BF16)
            kaug_ref[:, s * AUG + LANES:(s + 1) * AUG] = block_onehot
            vaug_ref[:, s * AUG:s * AUG + LANES] = jnp.where(head_a, vs, unit).astype(BF16)
            vaug_ref[:, s * AUG + LANES:(s + 1) * AUG] = jnp.where(
                head_a, pltpu.roll(vs, HEAD_DIM, axis=1), unit).astype(BF16)
    if tiles_per_seq is not None:
        nb = tm // MOBA_BLOCK
        km_ref[0] = jnp.mean(k_ref[...].reshape(nb, MOBA_BLOCK, D_MODEL), axis=1)


def _qkv_rope(x, g, w, inv, *, tm, seq, pos0):
    n = x.shape[0]
    nt = n // tm
    body = functools.partial(_qkv_body, tm=tm, tiles_per_seq=None if seq is None else seq // tm, pos0=pos0)
    out_specs = [_rows(tm, D_MODEL)] * 3
    out_shape = [jax.ShapeDtypeStruct((n, D_MODEL), F32)] * 3
    if seq is not None:
        assert seq // MOBA_BLOCK <= LANES and tm % MOBA_BLOCK == 0
        nb = tm // MOBA_BLOCK
        out_specs += [_rows(tm, N_STRIPS * AUG)] * 2
        out_shape += [jax.ShapeDtypeStruct((n, N_STRIPS * AUG), BF16)] * 2
        out_specs.append(pl.BlockSpec((1, nb, D_MODEL), lambda t: (t, 0, 0)))
        out_shape.append(jax.ShapeDtypeStruct((nt, nb, D_MODEL), F32))
    return pl.pallas_call(
        body,
        grid=(nt,),
        in_specs=[_rows(tm, D_MODEL), _resident(g.shape), _resident(w.shape), _resident(inv.shape)],
        out_specs=out_specs,
        out_shape=out_shape,
        compiler_params=_params("parallel"),
        name="qkv_rope",
    )(x, g, w, inv)


def _top_blocks(gate, valid, blk, n_blk):
    g = jnp.where(valid, gate, -jnp.inf)
    sel = jnp.zeros(g.shape, jnp.bool_)
    for _ in range(MOBA_TOP):
        m = jnp.max(g, axis=-1, keepdims=True)
        idx = jnp.min(jnp.where(g == m, blk, float(n_blk)), axis=-1, keepdims=True)
        pick = blk == idx
        sel = jnp.logical_or(sel, pick)
        g = jnp.where(pick, -jnp.inf, g)
    return jnp.logical_and(sel, valid)


def _moba_prompt_body(q_ref, kaug_ref, vaug_ref, km_ref, o_ref, s_ref, m_ref, *, n_blk):
    i = pl.program_id(2)
    tq = MOBA_BLOCK
    q = q_ref[...]
    lane = _lane_iota((1, LANES))
    head_a = lane < HEAD_DIM
    q_heads = (jnp.where(head_a, q, 0.0), jnp.where(head_a, 0.0, q))

    km = jnp.concatenate([km_ref[0], jnp.zeros((LANES - n_blk, LANES), F32)], axis=0)
    blk = lane.astype(F32)
    valid = blk < i.astype(F32)
    qs, qaug = [], []
    for qh in q_heads:
        sel = _top_blocks(_dot_nt(qh, km, precision=lax.Precision.HIGHEST), valid, blk, LANES)
        qs.append((qh * HEAD_DIM ** -0.5).astype(BF16))
        qaug.append(jnp.concatenate([qs[-1], jnp.where(sel, 0.0, NEG_BIG).astype(BF16)], axis=1))

    def key_rows(g):
        return pl.ds(pl.multiple_of(g * (KEY_GROUP * MOBA_BLOCK), KEY_GROUP * MOBA_BLOCK), KEY_GROUP * MOBA_BLOCK)

    def lane_blocks_max(s):
        m = s[:, :LANES]
        for c in range(1, s.shape[1] // LANES):
            m = jnp.maximum(m, s[:, c * LANES:(c + 1) * LANES])
        return m

    def groups(n_blocks):
        return lax.shift_right_logical(n_blocks + (KEY_GROUP - 1), KEY_GROUP.bit_length() - 1)

    m_ref[...] = jnp.full(m_ref.shape, NEG_BIG, F32)

    def scores(g, carry):
        kg = kaug_ref[key_rows(g), :]
        for h in range(HEADS_PER_STRIP):
            s = _dot_nt(qaug[h], kg)
            for c in range(KEY_GROUP):
                s_ref[h, g * KEY_GROUP + c] = s[:, c * MOBA_BLOCK:(c + 1) * MOBA_BLOCK]
            m_ref[h] = jnp.maximum(m_ref[h], lane_blocks_max(s))
        return carry

    lax.fori_loop(0, groups(i), scores, 0)

    causal = lax.broadcasted_iota(jnp.int32, (tq, tq), 1) <= lax.broadcasted_iota(jnp.int32, (tq, tq), 0)
    ki = kaug_ref[pl.ds(pl.multiple_of(i * MOBA_BLOCK, MOBA_BLOCK), MOBA_BLOCK), 0:LANES]
    for h in range(HEADS_PER_STRIP):
        s = jnp.where(causal, _dot_nt(qs[h], ki), NEG_BIG)
        s_ref[h, i] = s
        for c in range(1, KEY_GROUP):
            s_ref[h, i + c] = jnp.full((tq, MOBA_BLOCK), NEG_BIG, F32)
        row_max = jnp.max(jnp.maximum(m_ref[h], lane_blocks_max(s)), axis=-1, keepdims=True)
        m_ref[h] = jnp.broadcast_to(row_max, (tq, LANES))

    outs = []
    for h in range(HEADS_PER_STRIP):
        def attend(g, acc, h=h):
            m = m_ref[h]
            p = []
            for c in range(KEY_GROUP):
                s = s_ref[h, g * KEY_GROUP + c]
                p += [jnp.exp(s[:, :LANES] - m), jnp.exp(s[:, LANES:] - m)]
            p = jnp.concatenate(p, axis=1).astype(BF16)
            return acc + _dot(p, vaug_ref[key_rows(g), h * LANES:(h + 1) * LANES])

        acc = lax.fori_loop(0, groups(i + 1), attend, jnp.zeros((tq, LANES), F32))
        denom = jnp.sum(jnp.where(lane == HEAD_DIM, acc, 0.0), axis=-1, keepdims=True)
        outs.append(acc / denom)
    o_ref[...] = jnp.where(head_a, outs[0], pltpu.roll(outs[1], HEAD_DIM, axis=1)).astype(o_ref.dtype)


def _moba_prompt(q, kaug, vaug, kmean, *, batch, seq):
    n = q.shape[0]
    n_blk = seq // MOBA_BLOCK
    assert n_blk % KEY_GROUP == 0 and n_blk <= LANES
    body = functools.partial(_moba_prompt_body, n_blk=n_blk)
    return pl.pallas_call(
        body,
        grid=(batch, N_STRIPS, n_blk),
        in_specs=[pl.BlockSpec((MOBA_BLOCK, LANES), lambda b, s, i: (b * n_blk + i, s)),
                  pl.BlockSpec((seq, AUG), lambda b, s, i: (b, s)),
                  pl.BlockSpec((seq, AUG), lambda b, s, i: (b, s)),
                  pl.BlockSpec((1, n_blk, LANES), lambda b, s, i: (b, 0, s))],
        out_specs=pl.BlockSpec((MOBA_BLOCK, LANES), lambda b, s, i: (b * n_blk + i, s)),
        out_shape=jax.ShapeDtypeStruct((n, D_MODEL), BF16),
        scratch_shapes=[pltpu.VMEM((HEADS_PER_STRIP, n_blk + KEY_GROUP - 1, MOBA_BLOCK, MOBA_BLOCK), F32),
                        pltpu.VMEM((HEADS_PER_STRIP, MOBA_BLOCK, LANES), F32)],
        compiler_params=_params("parallel", "parallel", "arbitrary"),
        name="moba_prompt",
    )(q, kaug, vaug, kmean)


PAGES_PER_STEP = 8


def _kmean_pages_body(pt_ref, *refs):
    del pt_ref
    pages, o_ref = refs[:PAGES_PER_STEP], refs[PAGES_PER_STEP]
    for c in range(PAGES_PER_STEP // PAGES_PER_BLOCK):
        tot = None
        for pg in range(PAGES_PER_BLOCK):
            s = jnp.sum(pages[c * PAGES_PER_BLOCK + pg][0, 0], axis=0)
            tot = s if tot is None else tot + s
        o_ref[0, c] = tot / float(MOBA_BLOCK)


def _kmean_pages(cache, pt_flat, *, batch, n_pages):
    steps = n_pages // PAGES_PER_STEP
    bps = PAGES_PER_STEP // PAGES_PER_BLOCK

    def page_spec(c):
        return pl.BlockSpec((1, 1, PAGE_SIZE, N_HEADS, HEAD_DIM),
                            lambda b, s, pt: (0, pt[b * n_pages + s * PAGES_PER_STEP + c], 0, 0, 0))

    return pl.pallas_call(
        _kmean_pages_body,
        grid_spec=pltpu.PrefetchScalarGridSpec(
            num_scalar_prefetch=1,
            grid=(batch, steps),
            in_specs=[page_spec(c) for c in range(PAGES_PER_STEP)],
            out_specs=pl.BlockSpec((1, bps, N_HEADS, HEAD_DIM), lambda b, s, pt: (b, s, 0, 0)),
        ),
        out_shape=jax.ShapeDtypeStruct((batch, steps * bps, N_HEADS, HEAD_DIM), F32),
        compiler_params=_params("parallel", "parallel"),
        name="kmean_pages",
    )(pt_flat, *([cache] * PAGES_PER_STEP))


def _select_body(q_ref, km_ref, sel_ref, *, n_blk):
    g = jnp.sum(km_ref[0] * q_ref[...], axis=-1, keepdims=True)
    blk = lax.broadcasted_iota(jnp.int32, (n_blk, 1, 1), 0).astype(F32)
    for r in range(MOBA_TOP):
        m = jnp.max(g, axis=0, keepdims=True)
        idx = jnp.min(jnp.where(g == m, blk, float(n_blk)), axis=0, keepdims=True)
        sel_ref[0, r] = idx[0].astype(jnp.int32)
        g = jnp.where(blk == idx, -jnp.inf, g)


def _select_blocks(qh, kmean):
    batch, n_blk = kmean.shape[:2]
    return pl.pallas_call(
        functools.partial(_select_body, n_blk=n_blk),
        grid=(batch,),
        in_specs=[pl.BlockSpec((1, N_HEADS, HEAD_DIM), lambda b: (b, 0, 0)),
                  pl.BlockSpec((1, n_blk, N_HEADS, HEAD_DIM), lambda b: (b, 0, 0, 0))],
        out_specs=pl.BlockSpec((1, MOBA_TOP, N_HEADS, 1), lambda b: (b, 0, 0, 0)),
        out_shape=jax.ShapeDtypeStruct((batch, MOBA_TOP, N_HEADS, 1), jnp.int32),
        compiler_params=_params("parallel"),
        name="select_blocks",
    )(qh, kmean)


STRIPS_PER_HEAD = MOBA_TOP * PAGES_PER_BLOCK
MXU_ROWS = 8


HEAD_GROUP = 8


def _attend_sample_body(pt_ref, sel_ref, q_ref, kn_ref, vn_ref, *refs):
    del pt_ref, sel_ref
    n = STRIPS_PER_HEAD
    k_refs, v_refs, o_ref = refs[:n], refs[n:2 * n], refs[2 * n]
    hh = pl.program_id(1) % HEAD_GROUP
    mine = lax.broadcasted_iota(jnp.int32, (HEAD_GROUP, 1), 0) == hh
    mine_page = lax.broadcasted_iota(jnp.int32, (1, HEAD_GROUP, 1), 1) == hh
    full = lax.Precision.HIGHEST

    def own_row(ref):
        return jnp.sum(jnp.where(mine, ref[0], 0.0), axis=0, keepdims=True)

    def page_rows(ref):
        return jnp.sum(jnp.where(mine_page, ref[0, 0], 0.0), axis=1)

    q = own_row(q_ref) * HEAD_DIM ** -0.5
    kn = own_row(kn_ref)
    vn = own_row(vn_ref)
    q8 = jnp.broadcast_to(q, (MXU_ROWS, HEAD_DIM))
    scores = [_dot_nt(q8, page_rows(k_refs[c]), precision=full)[0:1, :] for c in range(n)]
    s_own = jnp.sum(q * kn, axis=-1, keepdims=True)
    m = s_own
    for s in scores:
        m = jnp.maximum(m, jnp.max(s, axis=-1, keepdims=True))
    p_own = jnp.exp(s_own - m)
    l = p_own
    o = p_own * vn
    for c, s in enumerate(scores):
        p = jnp.exp(s - m)
        l = l + jnp.sum(p, axis=-1, keepdims=True)
        p8 = jnp.broadcast_to(p, (MXU_ROWS, PAGE_SIZE))
        o = o + jnp.dot(p8, page_rows(v_refs[c]), precision=full, preferred_element_type=F32)[0:1, :]

    @pl.when(hh == 0)
    def _():
        o_ref[0] = jnp.zeros((HEAD_GROUP, HEAD_DIM), F32)

    o_ref[0] = jnp.where(mine, o / l, o_ref[0])


def _attend_sample(qh, knh, vnh, cache_k, cache_v, pt_flat, sel_flat, *, batch, n_pages):
    def page_spec(c):
        sidx, pg = divmod(c, PAGES_PER_BLOCK)

        def index(b, h, pt, sel):
            blk = sel[(b * N_HEADS + h) * MOBA_TOP + sidx]
            return (0, pt[b * n_pages + blk * PAGES_PER_BLOCK + pg], 0, h // HEAD_GROUP, 0)
        return pl.BlockSpec((1, 1, PAGE_SIZE, HEAD_GROUP, HEAD_DIM), index)

    pages = [page_spec(c) for c in range(STRIPS_PER_HEAD)]
    group = pl.BlockSpec((1, HEAD_GROUP, HEAD_DIM), lambda b, h, pt, sel: (b, h // HEAD_GROUP, 0))
    return pl.pallas_call(
        _attend_sample_body,
        grid_spec=pltpu.PrefetchScalarGridSpec(
            num_scalar_prefetch=2,
            grid=(batch, N_HEADS),
            in_specs=[group, group, group] + pages + pages,
            out_specs=group,
        ),
        out_shape=jax.ShapeDtypeStruct((batch, N_HEADS, HEAD_DIM), F32),
        compiler_params=_params("parallel", "arbitrary"),
        name="attend_sample",
    )(pt_flat, sel_flat, qh, knh, vnh, *([cache_k] * len(pages)), *([cache_v] * len(pages)))


def _oproj_router_body(x_ref, o_ref, wo_ref, g_ref, wr_ref, br_ref, xo_ref, hb_ref, comb_ref):
    x = x_ref[...] + _mm(o_ref[...], wo_ref[...])
    xo_ref[...] = x
    h = _rms(x, g_ref[...])
    hb_ref[...] = h.astype(BF16)
    lane = _lane_iota((1, LANES))
    lane_f = lane.astype(F32)
    valid = lane < N_EXPERTS
    logits = jnp.dot(h, wr_ref[...], precision=lax.Precision.HIGHEST, preferred_element_type=F32) + br_ref[...]
    logits = jnp.where(valid, logits, -jnp.inf)
    ex = jnp.exp(logits - jnp.max(logits, axis=-1, keepdims=True))
    probs = jnp.where(valid, ex / jnp.sum(ex, axis=-1, keepdims=True), -1.0)
    p1 = jnp.max(probs, axis=-1, keepdims=True)
    pick1 = lane_f == jnp.min(jnp.where(probs == p1, lane_f, float(LANES)), axis=-1, keepdims=True)
    rest = jnp.where(pick1, -1.0, probs)
    p2 = jnp.max(rest, axis=-1, keepdims=True)
    pick2 = lane_f == jnp.min(jnp.where(rest == p2, lane_f, float(LANES)), axis=-1, keepdims=True)
    tot = p1 + p2
    comb_ref[...] = jnp.where(pick1, p1 / tot, 0.0) + jnp.where(pick2, p2 / tot, 0.0)


def _oproj_router(x, o, wo, g, wr, br, *, tm):
    n = x.shape[0]
    return pl.pallas_call(
        _oproj_router_body,
        grid=(n // tm,),
        in_specs=[_rows(tm, D_MODEL), _rows(tm, D_MODEL), _resident(wo.shape), _resident(g.shape),
                  _resident(wr.shape), _resident(br.shape)],
        out_specs=[_rows(tm, D_MODEL), _rows(tm, D_MODEL), _rows(tm, LANES)],
        out_shape=[jax.ShapeDtypeStruct((n, D_MODEL), F32), jax.ShapeDtypeStruct((n, D_MODEL), BF16),
                   jax.ShapeDtypeStruct((n, LANES), F32)],
        compiler_params=_params("parallel"),
        name="oproj_router",
    )(x, o, wo, g, wr, br)


def _moe_body(h_ref, x_ref, comb_ref, wg_ref, wu_ref, wd_ref, o_ref, acc_ref):
    e = pl.program_id(1)
    f = pl.program_id(2)

    @pl.when(jnp.logical_and(e == 0, f == 0))
    def _():
        acc_ref[...] = jnp.zeros(acc_ref.shape, F32)

    h = h_ref[...]
    a = _silu(_dot(h, wg_ref[0])) * _dot(h, wu_ref[0])
    y = _dot(a.astype(BF16), wd_ref[0])
    weight = jnp.sum(jnp.where(_lane_iota((1, LANES)) == e, comb_ref[...], 0.0), axis=-1, keepdims=True)
    acc_ref[...] += weight * y

    @pl.when(jnp.logical_and(e == pl.num_programs(1) - 1, f == pl.num_programs(2) - 1))
    def _():
        o_ref[...] = x_ref[...] + acc_ref[...]


def _moe(hb, x, comb, wg, wu, wd, *, tm):
    n = x.shape[0]
    nf = D_EXPERT // EXPERT_TILE
    return pl.pallas_call(
        _moe_body,
        grid=(n // tm, N_EXPERTS, nf),
        in_specs=[pl.BlockSpec((tm, D_MODEL), lambda t, e, f: (t, 0)),
                  pl.BlockSpec((tm, D_MODEL), lambda t, e, f: (t, 0)),
                  pl.BlockSpec((tm, LANES), lambda t, e, f: (t, 0)),
                  pl.BlockSpec((1, D_MODEL, EXPERT_TILE), lambda t, e, f: (e, 0, f)),
                  pl.BlockSpec((1, D_MODEL, EXPERT_TILE), lambda t, e, f: (e, 0, f)),
                  pl.BlockSpec((1, EXPERT_TILE, D_MODEL), lambda t, e, f: (e, f, 0))],
        out_specs=pl.BlockSpec((tm, D_MODEL), lambda t, e, f: (t, 0)),
        out_shape=jax.ShapeDtypeStruct((n, D_MODEL), F32),
        scratch_shapes=[pltpu.VMEM((tm, D_MODEL), F32)],
        compiler_params=_params("parallel", "arbitrary", "arbitrary"),
        name="moe",
    )(hb, x, comb, wg, wu, wd)


def kernel(x_prompt, x_sample, state_conv, state_pool, cache_k, cache_v, page_table, p_prompt, p_sample,
           g_mix, g_ffn, g_ple, g_final, w_in_sp, conv_w, w_pool_grp, pool_scale, w_out_sp, w_ffn_gate,
           w_ffn_up, w_ffn_down, w_qkv, w_o, w_router, b_router, w_exp_gate, w_exp_up, w_exp_down,
           w_ple_proj, w_ple_gate):
    batch, seq, _ = x_prompt.shape
    dec_batch = x_sample.shape[0]
    n_pages = page_table.shape[1]
    past = n_pages * PAGE_SIZE
    n_p = batch * seq

    row = lambda a: a.reshape(1, -1)
    bf = lambda a: a.astype(BF16)
    win, wpool, wout = bf(w_in_sp[0]), bf(w_pool_grp[0]), bf(w_out_sp[0])
    wfg, wfu, wfd = bf(w_ffn_gate[0]), bf(w_ffn_up[0]), bf(w_ffn_down[0])
    wqkv, wo = bf(w_qkv[0]), bf(w_o[0])
    weg, weu, wed = bf(w_exp_gate[0]), bf(w_exp_up[0]), bf(w_exp_down[0])
    wpp, wpg = bf(w_ple_proj), bf(w_ple_gate)
    wr = jnp.zeros((D_MODEL, LANES), F32).at[:, :N_EXPERTS].set(w_router[0])
    br = jnp.zeros((1, LANES), F32).at[:, :N_EXPERTS].set(b_router[0][None, :])
    cw, ps = conv_w[0], row(pool_scale[0])
    half = HEAD_DIM // 2
    inv = np.float32(ROPE_THETA) ** (-np.arange(half, dtype=np.float32) / np.float32(half))
    inv = jnp.asarray(np.tile(inv, LANES // half).reshape(1, LANES))

    tm = 512
    xp = x_prompt.reshape(n_p, D_MODEL)
    xp, ut, bt = _mixer_prompt(xp, row(g_mix[0]), win, cw, wpool, ps, wout, tm=tm, seq=seq)
    last = np.arange(batch) * (seq // tm) + seq // tm - 1
    conv_p = ut[last, HALO - (CONV_WIDTH - 1):, :][None]
    pool_p = bt[last, HALO - POOL_STATE:, :][None]
    xp = _ffn_ple(xp, p_prompt[0].reshape(n_p, D_PLE), row(g_ffn[0]), wfg, wfu, wfd, row(g_ple[0]),
                  wpp[0], wpg[0], tm=tm)
    q, k, v, kaug, vaug, kmean = _qkv_rope(xp, row(g_mix[1]), wqkv, inv, tm=tm, seq=seq, pos0=0)
    o = _moba_prompt(q, kaug, vaug, kmean.reshape(batch, seq // MOBA_BLOCK, D_MODEL), batch=batch, seq=seq)
    xp, hb, comb = _oproj_router(xp, o, wo, row(g_ffn[1]), wr, br, tm=tm)
    xp = _moe(hb, xp, comb, weg, weu, wed, tm=1024)
    y_p = _ple_final(xp, p_prompt[1].reshape(n_p, D_PLE), row(g_ple[1]), wpp[1], wpg[1], row(g_final), tm=tm)
    k_p = k.reshape(1, batch, seq, N_HEADS, HEAD_DIM)
    v_p = v.reshape(1, batch, seq, N_HEADS, HEAD_DIM)

    ns = dec_batch
    xs = x_sample.reshape(ns, D_MODEL)
    xs, conv_s, pool_s = _mixer_sample(xs, state_conv[0].reshape(ns, (CONV_WIDTH - 1) * D_CONV),
                                       state_pool[0].reshape(ns, POOL_STATE * D_POOL), row(g_mix[0]),
                                       w_in_sp[0], cw, w_pool_grp[0], ps, w_out_sp[0])
    conv_s = conv_s.reshape(1, ns, CONV_WIDTH - 1, D_CONV)
    pool_s = pool_s.reshape(1, ns, POOL_STATE, D_POOL)
    xs = _ffn_ple(xs, p_sample[0].reshape(ns, D_PLE), row(g_ffn[0]), w_ffn_gate[0], w_ffn_up[0], w_ffn_down[0],
                  row(g_ple[0]), w_ple_proj[0], w_ple_gate[0], tm=ns)
    qs, ks, vs = _qkv_rope(xs, row(g_mix[1]), w_qkv[0], inv, tm=ns, seq=None, pos0=past)
    pt_flat = page_table.reshape(-1)
    heads = lambda a: a.reshape(ns, N_HEADS, HEAD_DIM)
    kmean_s = _kmean_pages(cache_k, pt_flat, batch=ns, n_pages=n_pages)
    sel = _select_blocks(heads(qs), kmean_s)
    sel_flat = jnp.transpose(sel[..., 0], (0, 2, 1)).reshape(-1)
    os_ = _attend_sample(heads(qs), heads(ks), heads(vs), cache_k, cache_v, pt_flat, sel_flat,
                         batch=ns, n_pages=n_pages)
    xs, hbs, combs = _oproj_router(xs, os_.reshape(ns, D_MODEL), w_o[0], row(g_ffn[1]), wr, br, tm=ns)
    xs = _moe(hbs, xs, combs, weg, weu, wed, tm=ns)
    y_s = _ple_final(xs, p_sample[1].reshape(ns, D_PLE), row(g_ple[1]), wpp[1], wpg[1], row(g_final), tm=ns)
    k_s = ks.reshape(1, ns, 1, N_HEADS, HEAD_DIM)
    v_s = vs.reshape(1, ns, 1, N_HEADS, HEAD_DIM)

    return (y_p.reshape(batch, seq, D_MODEL), y_s.reshape(ns, 1, D_MODEL), conv_p, conv_s, pool_p, pool_s,
            k_p, v_p, k_s, v_s)
```
